```python
import math
import jax, jax.numpy as jnp
from jax import lax
import numpy as np

D_MODEL = 1024
BATCH = 1
SEQ = 16384
DEPTH = 2
DEC_BATCH = 8
DEC_SEQ = 64
PAST_LEN = 4096

CHUNK = 64
N_A = DEPTH // 2
N_B = DEPTH - N_A
CONV_W = 3
N_HEADS = 8
QK_NOPE = 128
QK_ROPE = 64
V_DIM = 128
Q_LORA = 512
KV_LORA = 256
ROPE_BASE = 10000.0
ATTN_SCALE = (QK_NOPE + QK_ROPE) ** -0.5
Q_BLOCK = 128
PEER_HEADS = 8
N_KEYS = 128
N_EXPERTS = N_KEYS * N_KEYS
D_KEY = 256
PEER_TOPK = 16
PEER_BLOCK = 128
D_PLE = 256
EPS = 1e-6

kernel_name = "yoco_shortconv_mla_peer_stream_step"


def rms_norm(x, g):
    xf = x.astype(jnp.float32)
    y = xf * lax.rsqrt(jnp.mean(xf * xf, axis=-1, keepdims=True) + EPS)
    return (y * g.astype(jnp.float32)).astype(x.dtype)


def rope_tables(pos):
    inv = ROPE_BASE ** (-jnp.arange(0, QK_ROPE, 2, dtype=jnp.float32) / QK_ROPE)
    ang = pos.astype(jnp.float32)[:, None] * inv[None, :]
    return jnp.cos(ang), jnp.sin(ang)


def apply_rope(x, cos, sin):
    xf = x.astype(jnp.float32)
    x1, x2 = xf[..., :QK_ROPE // 2], xf[..., QK_ROPE // 2:]
    return jnp.concatenate([x1 * cos - x2 * sin, x2 * cos + x1 * sin], axis=-1).astype(x.dtype)


def short_conv_mixer(h, prev, w_in, conv_k, w_out):
    b_gate, c_gate, v = jnp.split(h @ w_in, 3, axis=-1)
    u = c_gate * v
    u_pad = jnp.concatenate([prev.astype(u.dtype), u], axis=1)
    T = h.shape[1]
    conv = conv_k[0] * u_pad[:, 0:T] + conv_k[1] * u_pad[:, 1:T + 1] + conv_k[2] * u_pad[:, 2:T + 2]
    return (b_gate * conv) @ w_out, u_pad[:, -(CONV_W - 1):]


def shared_latent(xs, pos, g_in, w_dkv_kr, g_ckv, g_kpe):
    h = rms_norm(xs, g_in)
    ck = h @ w_dkv_kr
    c_kv = rms_norm(ck[..., :KV_LORA], g_ckv)
    cos, sin = rope_tables(pos)
    k_pe = apply_rope(rms_norm(ck[..., KV_LORA:], g_kpe), cos, sin)
    return c_kv, k_pe


def expand_latent(c_kv, w_ukv, g_knope):
    kv = (c_kv @ w_ukv).reshape(c_kv.shape[0], c_kv.shape[1], N_HEADS, QK_NOPE + V_DIM)
    return rms_norm(kv[..., :QK_NOPE], g_knope), kv[..., QK_NOPE:]


def mla_queries(h, pos, w_dq, g_qa, w_uq, g_qnope, g_qpe):
    cq = rms_norm(h @ w_dq, g_qa)
    q = (cq @ w_uq).reshape(h.shape[0], h.shape[1], N_HEADS, QK_NOPE + QK_ROPE)
    cos, sin = rope_tables(pos)
    q_nope = rms_norm(q[..., :QK_NOPE], g_qnope)
    q_pe = apply_rope(rms_norm(q[..., QK_NOPE:], g_qpe), cos[:, None, :], sin[:, None, :])
    return q_nope, q_pe


def attend(q_nope, q_pe, k_nope, k_pe, v, mask):
    s = (jnp.einsum('bqhd,bkhd->bhqk', q_nope, k_nope)
         + jnp.einsum('bqhr,bkr->bhqk', q_pe, k_pe)).astype(jnp.float32) * ATTN_SCALE
    if mask is not None:
        s = jnp.where(mask, s, -jnp.inf)
    p = jax.nn.softmax(s, axis=-1).astype(v.dtype)
    return jnp.einsum('bhqk,bkhd->bqhd', p, v)


def prompt_attention(q_nope, q_pe, k_nope, k_pe, v):
    B, S = q_nope.shape[:2]
    nb = S // Q_BLOCK
    qn = q_nope.reshape(B, nb, Q_BLOCK, N_HEADS, QK_NOPE).transpose(1, 0, 2, 3, 4)
    qp = q_pe.reshape(B, nb, Q_BLOCK, N_HEADS, QK_ROPE).transpose(1, 0, 2, 3, 4)
    key_chunk = jnp.arange(S) // CHUNK

    def block(args):
        i, qn_b, qp_b = args
        q_chunk = (i * Q_BLOCK + jnp.arange(Q_BLOCK)) // CHUNK
        mask = key_chunk[None, :] <= q_chunk[:, None]
        return attend(qn_b, qp_b, k_nope, k_pe, v, mask)

    o = lax.map(block, (jnp.arange(nb), qn, qp))
    return o.transpose(1, 0, 2, 3, 4).reshape(B, S, N_HEADS, V_DIM)


def peer(h, w_query, sub_keys, u_emb, v_emb):
    B, T, D = h.shape
    n = B * T
    nb = -(-n // PEER_BLOCK)
    flat = jnp.pad(h.reshape(n, D), ((0, nb * PEER_BLOCK - n), (0, 0))).reshape(nb, PEER_BLOCK, D)

    def block(xb):
        q = (xb @ w_query).reshape(PEER_BLOCK, PEER_HEADS, 2, D_KEY // 2)
        s = jnp.einsum('thpc,hpnc->thpn', q, sub_keys).astype(jnp.float32)
        s_top, i_top = lax.top_k(s, PEER_TOPK)
        cand = (s_top[:, :, 0, :, None] + s_top[:, :, 1, None, :]).reshape(PEER_BLOCK, PEER_HEADS, PEER_TOPK * PEER_TOPK)
        c_top, c_idx = lax.top_k(cand, PEER_TOPK)
        ia = jnp.take_along_axis(i_top[:, :, 0], c_idx // PEER_TOPK, axis=-1)
        ib = jnp.take_along_axis(i_top[:, :, 1], c_idx % PEER_TOPK, axis=-1)
        expert = ia * N_KEYS + ib
        g = jax.nn.softmax(c_top, axis=-1)
        act = jax.nn.gelu(jnp.einsum('td,thkd->thk', xb, u_emb[expert]).astype(jnp.float32), approximate=False)
        w = (g * act).astype(xb.dtype)
        return jnp.einsum('thk,thkd->td', w, v_emb[expert])

    out = lax.map(block, flat).reshape(nb * PEER_BLOCK, D)[:n]
    return out.reshape(B, T, D)


def per_layer_embed(x, p, g_norm, w_gate, w_proj):
    gate = jax.nn.sigmoid((rms_norm(x, g_norm) @ w_gate).astype(jnp.float32)).astype(x.dtype)
    return x + (p @ w_proj) * gate


def setup_inputs(seed: int = 0) -> dict:
    key = jax.random.key(seed)
    ks = iter(jax.random.split(key, 40))
    f32 = jnp.float32

    def nrm(shape, scale):
        return jax.random.normal(next(ks), shape, f32) * scale

    def gain(shape):
        return 1.0 + 0.01 * jax.random.normal(next(ks), shape, f32)

    D = D_MODEL
    return {
        "x_prompt": nrm((BATCH, SEQ, D), 1.0),
        "x_sample": nrm((DEC_BATCH, DEC_SEQ, D), 1.0),
        "state_conv": nrm((N_A, DEC_BATCH, CONV_W - 1, D), 1.0),
        "cache_ckv": nrm((DEC_BATCH, PAST_LEN, KV_LORA), 1.0),
        "cache_kpe": nrm((DEC_BATCH, PAST_LEN, QK_ROPE), 1.0),
        "p_prompt": nrm((DEPTH, BATCH, SEQ, D_PLE), 1.0),
        "p_sample": nrm((DEPTH, DEC_BATCH, DEC_SEQ, D_PLE), 1.0),
        "norm_mix": gain((DEPTH, D)),
        "norm_ffn": gain((DEPTH, D)),
        "conv_w_in": nrm((N_A, D, 3 * D), D ** -0.5),
        "conv_kernel": nrm((N_A, CONV_W, D), CONV_W ** -0.5),
        "conv_w_out": nrm((N_A, D, D), D ** -0.5),
        "kv_norm_in": gain((D,)),
        "w_dkv_kr": nrm((D, KV_LORA + QK_ROPE), D ** -0.5),
        "ckv_norm": gain((KV_LORA,)),
        "kpe_norm": gain((QK_ROPE,)),
        "w_ukv": nrm((KV_LORA, N_HEADS * (QK_NOPE + V_DIM)), KV_LORA ** -0.5),
        "knope_norm": gain((QK_NOPE,)),
        "w_dq": nrm((N_B, D, Q_LORA), D ** -0.5),
        "q_a_norm": gain((N_B, Q_LORA)),
        "w_uq": nrm((N_B, Q_LORA, N_HEADS * (QK_NOPE + QK_ROPE)), Q_LORA ** -0.5),
        "qnope_norm": gain((N_B, QK_NOPE)),
        "qpe_norm": gain((N_B, QK_ROPE)),
        "w_o": nrm((N_B, N_HEADS * V_DIM, D), (N_HEADS * V_DIM) ** -0.5),
        "peer_w_query": nrm((DEPTH, D, PEER_HEADS * D_KEY), D ** -0.5),
        "peer_sub_keys": nrm((DEPTH, PEER_HEADS, 2, N_KEYS, D_KEY // 2), (D_KEY // 2) ** -0.5),
        "peer_u": nrm((DEPTH, N_EXPERTS, D), D ** -0.5),
        "peer_v": nrm((DEPTH, N_EXPERTS, D), PEER_HEADS ** -0.5),
        "ple_norm": gain((DEPTH, D)),
        "ple_w_gate": nrm((DEPTH, D, D), D ** -0.5),
        "ple_w_proj": nrm((DEPTH, D_PLE, D), D_PLE ** -0.5),
    }


def reference(x_prompt, x_sample, state_conv, cache_ckv, cache_kpe, p_prompt, p_sample,
              norm_mix, norm_ffn, conv_w_in, conv_kernel, conv_w_out,
              kv_norm_in, w_dkv_kr, ckv_norm, kpe_norm, w_ukv, knope_norm,
              w_dq, q_a_norm, w_uq, qnope_norm, qpe_norm, w_o,
              peer_w_query, peer_sub_keys, peer_u, peer_v,
              ple_norm, ple_w_gate, ple_w_proj):

    def run_group(x, p, pos, conv_prev, past_ckv, past_kpe):
        new_conv = []
        c_kv_new = k_pe_new = k_nope = k_pe = v = None
        B, T = x.shape[:2]
        for i in range(DEPTH):
            h = rms_norm(x, norm_mix[i])
            if i < N_A:
                y, st = short_conv_mixer(h, conv_prev[i], conv_w_in[i], conv_kernel[i], conv_w_out[i])
                new_conv.append(st)
            else:
                j = i - N_A
                if c_kv_new is None:
                    c_kv_new, k_pe_new = shared_latent(x, pos, kv_norm_in, w_dkv_kr, ckv_norm, kpe_norm)
                    if past_ckv is None:
                        c_all, k_pe = c_kv_new, k_pe_new
                    else:
                        c_all = jnp.concatenate([past_ckv.astype(x.dtype), c_kv_new], axis=1)
                        k_pe = jnp.concatenate([past_kpe.astype(x.dtype), k_pe_new], axis=1)
                    k_nope, v = expand_latent(c_all, w_ukv, knope_norm)
                q_nope, q_pe = mla_queries(h, pos, w_dq[j], q_a_norm[j], w_uq[j], qnope_norm[j], qpe_norm[j])
                if past_ckv is None:
                    o = prompt_attention(q_nope, q_pe, k_nope, k_pe, v)
                else:
                    o = attend(q_nope, q_pe, k_nope, k_pe, v, None)
                y = o.reshape(B, T, N_HEADS * V_DIM) @ w_o[j]
            x = x + y
            x = x + peer(rms_norm(x, norm_ffn[i]), peer_w_query[i], peer_sub_keys[i], peer_u[i], peer_v[i])
            x = per_layer_embed(x, p[i], ple_norm[i], ple_w_gate[i], ple_w_proj[i])
        return x, jnp.stack(new_conv, axis=0), c_kv_new, k_pe_new

    pos_p = jnp.arange(x_prompt.shape[1])
    pos_s = PAST_LEN + jnp.arange(x_sample.shape[1])
    zero_conv = jnp.zeros((N_A, x_prompt.shape[0], CONV_W - 1, D_MODEL), x_prompt.dtype)
    y_prompt, conv_p, ckv_p, kpe_p = run_group(x_prompt, p_prompt, pos_p, zero_conv, None, None)
    y_sample, conv_s, ckv_s, kpe_s = run_group(x_sample, p_sample, pos_s, state_conv, cache_ckv, cache_kpe)
    return (y_prompt, y_sample, conv_p, ckv_p, kpe_p, conv_s, ckv_s, kpe_s)
```

```python
import functools
import math

import jax
import jax.numpy as jnp
from jax import lax
from jax.experimental import pallas as pl
from jax.experimental.pallas import tpu as pltpu

D_MODEL = 1024
CHUNK = 64
N_HEADS = 8
QK_NOPE = 128
QK_ROPE = 64
V_DIM = 128
Q_LORA = 512
KV_LORA = 256
ROPE_BASE = 10000.0
ATTN_SCALE = (QK_NOPE + QK_ROPE) ** -0.5
PEER_HEADS = 8
N_KEYS = 128
N_EXPERTS = N_KEYS * N_KEYS
D_KEY = 256
PEER_TOPK = 16
D_PLE = 256
EPS = 1e-6

LANES = 128
QK_PAD = 256
VMEM_LIMIT = 56 * 1024 * 1024

F32 = jnp.float32
BF16 = jnp.bfloat16
NEG_INF = float("-inf")


def _params(sem):
    return pltpu.CompilerParams(dimension_semantics=sem, vmem_limit_bytes=VMEM_LIMIT)


def _rms(x, g):
    return x * lax.rsqrt(jnp.mean(x * x, axis=-1, keepdims=True) + EPS) * g


def _full(shape):
    nd = len(shape)
    return pl.BlockSpec(shape, lambda *_: (0,) * nd)


def _conv_body(x_ref, prev_ref, g_ref, win_ref, ck_ref, wout_ref, y_ref, st_ref, carry_ref, *, tb):
    @pl.when(pl.program_id(1) == 0)
    def _():
        carry_ref[...] = prev_ref[0]

    x = x_ref[0]
    h = _rms(x, g_ref[...]).astype(BF16)
    bcv = jnp.dot(h, win_ref[...], preferred_element_type=F32)
    b_gate = bcv[:, :D_MODEL]
    u = bcv[:, D_MODEL:2 * D_MODEL] * bcv[:, 2 * D_MODEL:]
    row = lax.broadcasted_iota(jnp.int32, (tb, D_MODEL), 0)
    p0 = carry_ref[0:1, :]
    p1 = carry_ref[1:2, :]
    u1 = jnp.where(row == 0, p1, pltpu.roll(u, 1, axis=0))
    u2 = jnp.where(row == 0, p0, jnp.where(row == 1, p1, pltpu.roll(u, 2, axis=0)))
    conv = ck_ref[0:1, :] * u2 + ck_ref[1:2, :] * u1 + ck_ref[2:3, :] * u
    y = jnp.dot((b_gate * conv).astype(BF16), wout_ref[...], preferred_element_type=F32)
    y_ref[0] = x + y
    last = u[tb - 2:tb, :]
    carry_ref[...] = last
    st_ref[0] = last


def _conv_mixer(x, prev, g, w_in, ck, w_out, tb):
    b, t, d = x.shape
    return pl.pallas_call(
        functools.partial(_conv_body, tb=tb),
        grid=(b, t // tb),
        in_specs=[
            pl.BlockSpec((1, tb, d), lambda i, j: (i, j, 0)),
            pl.BlockSpec((1, 2, d), lambda i, j: (i, 0, 0)),
            _full((1, d)), _full((d, 3 * d)), _full((3, d)), _full((d, d)),
        ],
        out_specs=[
            pl.BlockSpec((1, tb, d), lambda i, j: (i, j, 0)),
            pl.BlockSpec((1, 2, d), lambda i, j: (i, 0, 0)),
        ],
        out_shape=[jax.ShapeDtypeStruct((b, t, d), F32), jax.ShapeDtypeStruct((b, 2, d), F32)],
        scratch_shapes=[pltpu.VMEM((2, d), F32)],
        compiler_params=_params(("arbitrary", "arbitrary")),
        name="conv_mixer",
    )(x, prev, g, w_in, ck, w_out)


def _ple_body(x_ref, p_ref, g_ref, wg_ref, wp_ref, o_ref):
    x = x_ref[...]
    hn = _rms(x, g_ref[...]).astype(BF16)
    gate = jax.nn.sigmoid(jnp.dot(hn, wg_ref[...], preferred_element_type=F32))
    proj = jnp.dot(p_ref[...].astype(BF16), wp_ref[...], preferred_element_type=F32)
    o_ref[...] = x + proj * gate


def _ple(x, p, g, w_gate, w_proj, tb):
    n, d = x.shape
    return pl.pallas_call(
        _ple_body,
        grid=(n // tb,),
        in_specs=[
            pl.BlockSpec((tb, d), lambda i: (i, 0)),
            pl.BlockSpec((tb, D_PLE), lambda i: (i, 0)),
            _full((1, d)), _full((d, d)), _full((D_PLE, d)),
        ],
        out_specs=pl.BlockSpec((tb, d), lambda i: (i, 0)),
        out_shape=jax.ShapeDtypeStruct((n, d), F32),
        compiler_params=_params(("arbitrary",)),
        name="ple",
    )(x, p, g, w_gate, w_proj)


def _latent_body(x_ref, g_ref, w_ref, gc_ref, gr_ref, rot_ref, ckv_ref, kpe_ref):
    h = _rms(x_ref[...], g_ref[...]).astype(BF16)
    ck = jnp.dot(h, w_ref[...], preferred_element_type=F32)
    ckv_ref[...] = _rms(ck[:, :KV_LORA], gc_ref[...])
    r = _rms(ck[:, KV_LORA:], gr_ref[...]) * rot_ref[...]
    kpe_ref[...] = (r + pltpu.roll(r, QK_ROPE, axis=1))[:, :QK_ROPE]


def _latent(x, g, w, g_ckv, g_kpe, rot, tb):
    n, d = x.shape
    return pl.pallas_call(
        _latent_body,
        grid=(n // tb,),
        in_specs=[
            pl.BlockSpec((tb, d), lambda i: (i, 0)),
            _full((1, d)), _full((d, KV_LORA + LANES)), _full((1, KV_LORA)), _full((1, LANES)),
            pl.BlockSpec((tb, LANES), lambda i: (i, 0)),
        ],
        out_specs=[
            pl.BlockSpec((tb, KV_LORA), lambda i: (i, 0)),
            pl.BlockSpec((tb, QK_ROPE), lambda i: (i, 0)),
        ],
        out_shape=[jax.ShapeDtypeStruct((n, KV_LORA), F32), jax.ShapeDtypeStruct((n, QK_ROPE), F32)],
        compiler_params=_params(("arbitrary",)),
        name="latent",
    )(x, g, w, g_ckv, g_kpe, rot)


def _expand_body(c_ref, kpet_ref, wkt_ref, wv_ref, gk_ref, kt_ref, v_ref, *, tb):
    cb = c_ref[...].astype(BF16)
    v_ref[...] = jnp.dot(cb, wv_ref[...], preferred_element_type=F32).astype(BF16)
    kt = lax.dot_general(wkt_ref[...], cb, (((1,), (1,)), ((), ())), preferred_element_type=F32)
    kpet = kpet_ref[...].astype(BF16)
    for h in range(N_HEADS):
        k = kt[h * QK_NOPE:(h + 1) * QK_NOPE, :]
        ms = jnp.mean(k * k, axis=0, keepdims=True)
        kt_ref[h, 0:QK_NOPE, :] = (k * lax.rsqrt(ms + EPS) * gk_ref[...]).astype(BF16)
        kt_ref[h, QK_NOPE:QK_NOPE + QK_ROPE, :] = kpet
        kt_ref[h, QK_NOPE + QK_ROPE:, :] = jnp.zeros((QK_PAD - QK_NOPE - QK_ROPE, tb), BF16)


def _expand(c, kpet, wkt, wv, gk_t, tb):
    n = c.shape[0]
    return pl.pallas_call(
        functools.partial(_expand_body, tb=tb),
        grid=(n // tb,),
        in_specs=[
            pl.BlockSpec((tb, KV_LORA), lambda i: (i, 0)),
            pl.BlockSpec((QK_ROPE, tb), lambda i: (0, i)),
            _full((N_HEADS * QK_NOPE, KV_LORA)), _full((KV_LORA, N_HEADS * V_DIM)), _full((QK_NOPE, tb)),
        ],
        out_specs=[
            pl.BlockSpec((N_HEADS, QK_PAD, tb), lambda i: (0, 0, i)),
            pl.BlockSpec((tb, N_HEADS * V_DIM), lambda i: (i, 0)),
        ],
        out_shape=[jax.ShapeDtypeStruct((N_HEADS, QK_PAD, n), BF16),
                   jax.ShapeDtypeStruct((n, N_HEADS * V_DIM), BF16)],
        compiler_params=_params(("arbitrary",)),
        name="expand_latent",
    )(c, kpet, wkt, wv, gk_t)


def _query_body(x_ref, g_ref, wdq_ref, gqa_ref, wuq_ref, gn_ref, gr_ref, rot_ref, q_ref):
    h = _rms(x_ref[...], g_ref[...]).astype(BF16)
    cq = _rms(jnp.dot(h, wdq_ref[...], preferred_element_type=F32), gqa_ref[...]).astype(BF16)
    q = jnp.dot(cq, wuq_ref[...], preferred_element_type=F32)
    rot = rot_ref[...]
    for hd in range(N_HEADS):
        base = hd * QK_PAD
        qn = _rms(q[:, base:base + QK_NOPE], gn_ref[...])
        r = _rms(q[:, base + QK_NOPE:base + QK_PAD], gr_ref[...]) * rot
        qr = r + pltpu.roll(r, QK_ROPE, axis=1)
        q_ref[hd, :, 0:QK_NOPE] = (qn * ATTN_SCALE).astype(BF16)
        q_ref[hd, :, QK_NOPE:] = (qr * ATTN_SCALE).astype(BF16)


def _queries(x, g, w_dq, g_qa, w_uq, g_qn, g_qr, rot, tb):
    n, d = x.shape
    return pl.pallas_call(
        _query_body,
        grid=(n // tb,),
        in_specs=[
            pl.BlockSpec((tb, d), lambda i: (i, 0)),
            _full((1, d)), _full((d, Q_LORA)), _full((1, Q_LORA)), _full((Q_LORA, N_HEADS * QK_PAD)),
            _full((1, QK_NOPE)), _full((1, LANES)),
            pl.BlockSpec((tb, LANES), lambda i: (i, 0)),
        ],
        out_specs=pl.BlockSpec((N_HEADS, tb, QK_PAD), lambda i: (0, i, 0)),
        out_shape=jax.ShapeDtypeStruct((N_HEADS, n, QK_PAD), BF16),
        compiler_params=_params(("arbitrary",)),
        name="mla_queries",
    )(x, g, w_dq, g_qa, w_uq, g_qn, g_qr, rot)


def _attn_body(q_ref, kt_ref, v_ref, x_ref, wo_ref, o_ref, m_ref, l_ref, acc_ref, *, tq, tk, causal, tail_valid):
    i = pl.program_id(0)
    j = pl.program_id(1)
    last = i if causal else pl.num_programs(1) - 1

    @pl.when(j == 0)
    def _():
        m_ref[...] = jnp.full(m_ref.shape, NEG_INF, F32)
        l_ref[...] = jnp.zeros(l_ref.shape, F32)
        acc_ref[...] = jnp.zeros(acc_ref.shape, F32)

    def step(masked):
        if masked:
            kk = lax.broadcasted_iota(jnp.int32, (tq, tk), 1)
            if causal:
                qq = lax.broadcasted_iota(jnp.int32, (tq, tk), 0)
                keep = (kk // CHUNK) <= (qq // CHUNK)
            else:
                keep = kk < tail_valid
        for h in range(N_HEADS):
            s = jnp.dot(q_ref[h], kt_ref[h], preferred_element_type=F32)
            if masked:
                s = jnp.where(keep, s, NEG_INF)
            m_prev = m_ref[h]
            m_new = jnp.maximum(m_prev, jnp.max(s, axis=-1, keepdims=True))
            alpha = jnp.exp(m_prev - m_new)
            p = jnp.exp(s - m_new)
            l_ref[h] = alpha * l_ref[h] + jnp.sum(p, axis=-1, keepdims=True)
            cols = slice(h * V_DIM, (h + 1) * V_DIM)
            pv = jnp.dot(p.astype(BF16), v_ref[:, cols], preferred_element_type=F32)
            acc_ref[:, cols] = alpha * acc_ref[:, cols] + pv
            m_ref[h] = m_new

    @pl.when(j < last)
    def _():
        step(False)

    @pl.when(j == last)
    def _():
        step(True)
        for h in range(N_HEADS):
            cols = slice(h * V_DIM, (h + 1) * V_DIM)
            acc_ref[:, cols] = acc_ref[:, cols] / l_ref[h]
        o = acc_ref[...].astype(BF16)
        o_ref[...] = x_ref[...] + jnp.dot(o, wo_ref[...], preferred_element_type=F32)


def _attention(q, kt, v, x, w_o, *, tq, tk, causal, kv_tiles, tail_valid=0):
    n, d = x.shape
    if causal:
        kv_idx = lambda i, j: jnp.minimum(i, j)
    else:
        kv_idx = lambda i, j: i * kv_tiles + j
    return pl.pallas_call(
        functools.partial(_attn_body, tq=tq, tk=tk, causal=causal, tail_valid=tail_valid),
        grid=(n // tq, kv_tiles),
        in_specs=[
            pl.BlockSpec((N_HEADS, tq, QK_PAD), lambda i, j: (0, i, 0)),
            pl.BlockSpec((N_HEADS, QK_PAD, tk), lambda i, j: (0, 0, kv_idx(i, j))),
            pl.BlockSpec((tk, N_HEADS * V_DIM), lambda i, j: (kv_idx(i, j), 0)),
            pl.BlockSpec((tq, d), lambda i, j: (i, 0)),
            _full((N_HEADS * V_DIM, d)),
        ],
        out_specs=pl.BlockSpec((tq, d), lambda i, j: (i, 0)),
        out_shape=jax.ShapeDtypeStruct((n, d), F32),
        scratch_shapes=[
            pltpu.VMEM((N_HEADS, tq, 1), F32),
            pltpu.VMEM((N_HEADS, tq, 1), F32),
            pltpu.VMEM((tq, N_HEADS * V_DIM), F32),
        ],
        compiler_params=_params(("arbitrary", "arbitrary")),
        name="attention_causal" if causal else "attention_cached",
    )(q, kt, v, x, w_o)


def _extract_top(cur, rounds):
    rows = []
    for _ in range(rounds):
        m = jnp.max(cur, axis=0, keepdims=True)
        rows.append(m)
        cur = jnp.where(cur == m, NEG_INF, cur)
    return rows


def _peer_body(x_ref, g_ref, wqt_ref, keys_ref, u_ref, vt_ref, o_ref,
               xnt_ref, s_ref, top_ref, eb_ref, c_ref, tau_ref, act_ref, w_ref, acc_ref, *, tb, eb_rows):
    e = pl.program_id(1)
    nch = tb // LANES
    ipb = eb_rows // N_KEYS
    sub = 32

    @pl.when(e == 0)
    def _prologue():
        xn = _rms(x_ref[...], g_ref[...])
        xnt_ref[...] = xn.T.astype(BF16)
        qt = jnp.dot(wqt_ref[...], xnt_ref[...], preferred_element_type=F32)
        for hp in range(2 * PEER_HEADS):
            qs = qt[hp * N_KEYS:(hp + 1) * N_KEYS, :].astype(BF16)
            s_ref[hp] = jnp.dot(keys_ref[hp], qs, preferred_element_type=F32)

        def top_side(idx, carry):
            hp = idx // nch
            lanes = pl.ds(pl.multiple_of((idx % nch) * LANES, LANES), LANES)
            rows = _extract_top(s_ref[hp, :, lanes], PEER_TOPK)
            for r in range(PEER_TOPK):
                top_ref[hp, r:r + 1, lanes] = rows[r]
            return carry

        lax.fori_loop(0, 2 * PEER_HEADS * nch, top_side, 0)

        def top_pair(idx, carry):
            h = idx // nch
            lanes = pl.ds(pl.multiple_of((idx % nch) * LANES, LANES), LANES)
            a = top_ref[2 * h, :, lanes]
            b = top_ref[2 * h + 1, :, lanes]
            ri = lax.broadcasted_iota(jnp.int32, (PEER_TOPK, LANES), 0)
            pieces = []
            for k in range(4):
                pieces.append(jnp.where(ri < PEER_TOPK // (k + 1), a[k:k + 1, :] + b, NEG_INF))
            for l in range(3):
                ok = jnp.where(ri >= 4, ri, PEER_TOPK) < PEER_TOPK // (l + 1)
                pieces.append(jnp.where(ok, b[l:l + 1, :] + a, NEG_INF))
            cmax = a[0:1, :] + b[0:1, :]
            best = _extract_top(jnp.concatenate(pieces, axis=0), PEER_TOPK)
            z = jnp.zeros((1, LANES), F32)
            for m in best:
                z = z + jnp.exp(m - cmax)
            tau_ref[h, :, lanes] = best[-1]
            eb_ref[h, :, lanes] = jnp.exp(s_ref[2 * h + 1, :, lanes] - b[0:1, :])
            c_ref[h, :, lanes] = jnp.exp(s_ref[2 * h, :, lanes] - a[0:1, :]) / z
            return carry

        lax.fori_loop(0, PEER_HEADS * nch, top_pair, 0)
        acc_ref[...] = jnp.zeros(acc_ref.shape, F32)

    act_ref[...] = jnp.dot(u_ref[...], xnt_ref[...], preferred_element_type=F32)

    def route(idx, carry):
        grp = idx // nch
        lanes = pl.ds(pl.multiple_of((idx % nch) * LANES, LANES), LANES)
        first = pl.ds(pl.multiple_of(e * ipb + grp * 8, 8), 8)
        sa = [s_ref[2 * h, first, lanes] for h in range(PEER_HEADS)]
        cf = [c_ref[h, first, lanes] for h in range(PEER_HEADS)]
        tau = [tau_ref[h, :, lanes] for h in range(PEER_HEADS)]
        for r in range(8):
            for jb in range(N_KEYS // sub):
                rows = slice(jb * sub, (jb + 1) * sub)
                gsum = jnp.zeros((sub, LANES), F32)
                for h in range(PEER_HEADS):
                    hit = (s_ref[2 * h + 1, rows, lanes] + sa[h][r:r + 1, :]) >= tau[h]
                    gsum = gsum + jnp.where(hit, eb_ref[h, rows, lanes] * cf[h][r:r + 1, :], 0.0)
                erow = pl.ds(pl.multiple_of((grp * 8 + r) * N_KEYS + jb * sub, sub), sub)
                a = act_ref[erow, lanes]
                gelu = 0.5 * a * (1.0 + lax.erf(a * math.sqrt(0.5)))
                w_ref[erow, lanes] = (gelu * gsum).astype(BF16)
        return carry

    lax.fori_loop(0, (ipb // 8) * nch, route, 0)
    acc_ref[...] += jnp.dot(vt_ref[...], w_ref[...], preferred_element_type=F32)

    @pl.when(e == pl.num_programs(1) - 1)
    def _():
        o_ref[...] = x_ref[...] + acc_ref[...].T


def _peer(x, g, wqt, keys, u, vt, *, tb, eb_rows):
    n, d = x.shape
    return pl.pallas_call(
        functools.partial(_peer_body, tb=tb, eb_rows=eb_rows),
        grid=(n // tb, N_EXPERTS // eb_rows),
        in_specs=[
            pl.BlockSpec((tb, d), lambda i, e: (i, 0)),
            _full((1, d)), _full((PEER_HEADS * D_KEY, d)), _full((2 * PEER_HEADS, N_KEYS, D_KEY // 2)),
            pl.BlockSpec((eb_rows, d), lambda i, e: (e, 0)),
            pl.BlockSpec((d, eb_rows), lambda i, e: (0, e)),
        ],
        out_specs=pl.BlockSpec((tb, d), lambda i, e: (i, 0)),
        out_shape=jax.ShapeDtypeStruct((n, d), F32),
        scratch_shapes=[
            pltpu.VMEM((d, tb), BF16),
            pltpu.VMEM((2 * PEER_HEADS, N_KEYS, tb), F32),
            pltpu.VMEM((2 * PEER_HEADS, PEER_TOPK, tb), F32),
            pltpu.VMEM((PEER_HEADS, N_KEYS, tb), F32),
            pltpu.VMEM((PEER_HEADS, N_KEYS, tb), F32),
            pltpu.VMEM((PEER_HEADS, 1, tb), F32),
            pltpu.VMEM((eb_rows, tb), F32),
            pltpu.VMEM((eb_rows, tb), BF16),
            pltpu.VMEM((d, tb), F32),
        ],
        compiler_params=_params(("arbitrary", "arbitrary")),
        name="peer",
    )(x, g, wqt, keys, u, vt)


def _rot_table(pos):
    inv = ROPE_BASE ** (-jnp.arange(0, QK_ROPE, 2, dtype=F32) / QK_ROPE)
    ang = pos.astype(F32)[:, None] * inv[None, :]
    cos, sin = jnp.cos(ang), jnp.sin(ang)
    return jnp.concatenate([cos, cos, -sin, sin], axis=-1)


def _dup_rope(w):
    h = QK_ROPE // 2
    return jnp.concatenate([w[..., :h], w[..., h:], w[..., h:], w[..., :h]], axis=-1)


def kernel(x_prompt, x_sample, state_conv, cache_ckv, cache_kpe, p_prompt, p_sample, norm_mix, norm_ffn, conv_w_in, conv_kernel, conv_w_out, kv_norm_in, w_dkv_kr, ckv_norm, kpe_norm, w_ukv, knope_norm, w_dq, q_a_norm, w_uq, qnope_norm, qpe_norm, w_o, peer_w_query, peer_sub_keys, peer_u, peer_v, ple_norm, ple_w_gate, ple_w_proj):
    d = D_MODEL
    row = lambda a: a.reshape(1, -1)

    w_in = conv_w_in[0].astype(BF16)
    w_out = conv_w_out[0].astype(BF16)
    w_lat = jnp.concatenate([w_dkv_kr[:, :KV_LORA], _dup_rope(w_dkv_kr[:, KV_LORA:])], axis=1).astype(BF16)
    g_kpe = row(_dup_rope(kpe_norm))
    ukv = w_ukv.reshape(KV_LORA, N_HEADS, QK_NOPE + V_DIM)
    wkt = ukv[:, :, :QK_NOPE].reshape(KV_LORA, N_HEADS * QK_NOPE).T.astype(BF16)
    wv = ukv[:, :, QK_NOPE:].reshape(KV_LORA, N_HEADS * V_DIM).astype(BF16)
    uq = w_uq[0].reshape(Q_LORA, N_HEADS, QK_NOPE + QK_ROPE)
    w_uq_p = jnp.concatenate([uq[:, :, :QK_NOPE], _dup_rope(uq[:, :, QK_NOPE:])], axis=-1)
    w_uq_p = w_uq_p.reshape(Q_LORA, N_HEADS * QK_PAD).astype(BF16)
    g_qpe = row(_dup_rope(qpe_norm[0]))
    w_dq_b = w_dq[0].astype(BF16)
    w_o_b = w_o[0].astype(BF16)
    peer_w = []
    for i in range(2):
        peer_w.append((
            row(norm_ffn[i]),
            peer_w_query[i].T.astype(BF16),
            peer_sub_keys[i].reshape(2 * PEER_HEADS, N_KEYS, D_KEY // 2).astype(BF16),
            peer_u[i].astype(BF16),
            peer_v[i].T.astype(BF16),
        ))
    ple_w = [(row(ple_norm[i]), ple_w_gate[i].astype(BF16), ple_w_proj[i].astype(BF16)) for i in range(2)]

    def run_group(x, p, pos, conv_prev, past_ckv, past_kpe):
        b, t = x.shape[:2]
        n = b * t
        tb = min(512, n)
        x1, conv_state = _conv_mixer(x, conv_prev, row(norm_mix[0]), w_in, conv_kernel[0], w_out, min(512, t))
        xf = x1.reshape(n, d)
        xf = _peer(xf, *peer_w[0], tb=tb, eb_rows=1024)
        xf = _ple(xf, p[0].reshape(n, D_PLE), *ple_w[0], tb)
        rot = _rot_table(pos)
        ckv, kpe = _latent(xf, row(kv_norm_in), w_lat, row(ckv_norm), g_kpe, rot, tb)
        q = _queries(xf, row(norm_mix[1]), w_dq_b, row(q_a_norm[0]), w_uq_p, row(qnope_norm[0]), g_qpe, rot, tb)
        tk = 512
        gk_t = jnp.broadcast_to(knope_norm[:, None], (QK_NOPE, tk))
        if past_ckv is None:
            kt, v = _expand(ckv, kpe.T, wkt, wv, gk_t, tk)
            xf = _attention(q, kt, v, xf, w_o_b, tq=tk, tk=tk, causal=True, kv_tiles=n // tk)
        else:
            total = past_ckv.shape[1] + t
            tiles = -(-total // tk)
            pad = tiles * tk - total
            c_all = jnp.concatenate([past_ckv, ckv.reshape(b, t, KV_LORA), jnp.zeros((b, pad, KV_LORA), F32)], axis=1)
            r_all = jnp.concatenate([past_kpe, kpe.reshape(b, t, QK_ROPE), jnp.zeros((b, pad, QK_ROPE), F32)], axis=1)
            kt, v = _expand(c_all.reshape(b * tiles * tk, KV_LORA), r_all.reshape(b * tiles * tk, QK_ROPE).T, wkt, wv, gk_t, tk)
            xf = _attention(q, kt, v, xf, w_o_b, tq=t, tk=tk, causal=False, kv_tiles=tiles, tail_valid=tk - pad)
        xf = _peer(xf, *peer_w[1], tb=tb, eb_rows=1024)
        xf = _ple(xf, p[1].reshape(n, D_PLE), *ple_w[1], tb)
        return (xf.reshape(b, t, d), conv_state[None], ckv.reshape(b, t, KV_LORA), kpe.reshape(b, t, QK_ROPE))

    pos_p = jnp.arange(x_prompt.shape[1])
    pos_s = jnp.tile(cache_ckv.shape[1] + jnp.arange(x_sample.shape[1]), x_sample.shape[0])
    zero_conv = jnp.zeros((x_prompt.shape[0], 2, d), F32)
    y_p, conv_p, ckv_p, kpe_p = run_group(x_prompt, p_prompt, pos_p, zero_conv, None, None)
    y_s, conv_s, ckv_s, kpe_s = run_group(x_sample, p_sample, pos_s, state_conv[0], cache_ckv, cache_kpe)
    return (y_p, y_s, conv_p, ckv_p, kpe_p, conv_s, ckv_s, kpe_s)
```

```python
import functools
import math

import jax
import jax.numpy as jnp
from jax import lax
from jax.experimental import pallas as pl
from jax.experimental.pallas import tpu as pltpu

D_MODEL = 1024
CHUNK = 64
N_HEADS = 8
QK_NOPE = 128
QK_ROPE = 64
V_DIM = 128
Q_LORA = 512
KV_LORA = 256
ROPE_BASE = 10000.0
ATTN_SCALE = (QK_NOPE + QK_ROPE) ** -0.5
PEER_HEADS = 8
N_KEYS = 128
N_EXPERTS = N_KEYS * N_KEYS
D_KEY = 256
PEER_TOPK = 16
D_PLE = 256
EPS = 1e-6

LANES = 128
QK_PAD = 256
VMEM_LIMIT = 56 * 1024 * 1024

F32 = jnp.float32
BF16 = jnp.bfloat16
NEG_INF = float("-inf")


def _params(sem, flags=None):
    return pltpu.CompilerParams(dimension_semantics=sem, vmem_limit_bytes=VMEM_LIMIT, flags=flags)


def _rms(x, g):
    return x * lax.rsqrt(jnp.mean(x * x, axis=-1, keepdims=True) + EPS) * g


def _full(shape):
    nd = len(shape)
    return pl.BlockSpec(shape, lambda *_: (0,) * nd)


def _conv_body(x_ref, prev_ref, g_ref, win_ref, ck_ref, wout_ref, y_ref, st_ref, carry_ref, *, tb):
    @pl.when(pl.program_id(1) == 0)
    def _():
        carry_ref[...] = prev_ref[0]

    x = x_ref[0]
    h = _rms(x, g_ref[...]).astype(BF16)
    bcv = jnp.dot(h, win_ref[...], preferred_element_type=F32)
    b_gate = bcv[:, :D_MODEL]
    u = bcv[:, D_MODEL:2 * D_MODEL] * bcv[:, 2 * D_MODEL:]
    row = lax.broadcasted_iota(jnp.int32, (tb, D_MODEL), 0)
    p0 = carry_ref[0:1, :]
    p1 = carry_ref[1:2, :]
    u1 = jnp.where(row == 0, p1, pltpu.roll(u, 1, axis=0))
    u2 = jnp.where(row == 0, p0, jnp.where(row == 1, p1, pltpu.roll(u, 2, axis=0)))
    conv = ck_ref[0:1, :] * u2 + ck_ref[1:2, :] * u1 + ck_ref[2:3, :] * u
    y = jnp.dot((b_gate * conv).astype(BF16), wout_ref[...], preferred_element_type=F32)
    y_ref[0] = x + y
    last = u[tb - 2:tb, :]
    carry_ref[...] = last
    st_ref[0] = last


def _conv_mixer(x, prev, g, w_in, ck, w_out, tb):
    b, t, d = x.shape
    return pl.pallas_call(
        functools.partial(_conv_body, tb=tb),
        grid=(b, t // tb),
        in_specs=[
            pl.BlockSpec((1, tb, d), lambda i, j: (i, j, 0)),
            pl.BlockSpec((1, 2, d), lambda i, j: (i, 0, 0)),
            _full((1, d)), _full((d, 3 * d)), _full((3, d)), _full((d, d)),
        ],
        out_specs=[
            pl.BlockSpec((1, tb, d), lambda i, j: (i, j, 0)),
            pl.BlockSpec((1, 2, d), lambda i, j: (i, 0, 0)),
        ],
        out_shape=[jax.ShapeDtypeStruct((b, t, d), F32), jax.ShapeDtypeStruct((b, 2, d), F32)],
        scratch_shapes=[pltpu.VMEM((2, d), F32)],
        compiler_params=_params(("arbitrary", "arbitrary")),
        name="conv_mixer",
    )(x, prev, g, w_in, ck, w_out)


def _ple_body(x_ref, p_ref, g_ref, wg_ref, wp_ref, o_ref):
    x = x_ref[...]
    hn = _rms(x, g_ref[...]).astype(BF16)
    gate = jax.nn.sigmoid(jnp.dot(hn, wg_ref[...], preferred_element_type=F32))
    proj = jnp.dot(p_ref[...].astype(BF16), wp_ref[...], preferred_element_type=F32)
    o_ref[...] = x + proj * gate


def _ple(x, p, g, w_gate, w_proj, tb):
    n, d = x.shape
    return pl.pallas_call(
        _ple_body,
        grid=(n // tb,),
        in_specs=[
            pl.BlockSpec((tb, d), lambda i: (i, 0)),
            pl.BlockSpec((tb, D_PLE), lambda i: (i, 0)),
            _full((1, d)), _full((d, d)), _full((D_PLE, d)),
        ],
        out_specs=pl.BlockSpec((tb, d), lambda i: (i, 0)),
        out_shape=jax.ShapeDtypeStruct((n, d), F32),
        compiler_params=_params(("arbitrary",)),
        name="ple",
    )(x, p, g, w_gate, w_proj)


def _latent_body(x_ref, g_ref, w_ref, gc_ref, gr_ref, rot_ref, ckv_ref, kpe_ref):
    h = _rms(x_ref[...], g_ref[...]).astype(BF16)
    ck = jnp.dot(h, w_ref[...], preferred_element_type=F32)
    ckv_ref[...] = _rms(ck[:, :KV_LORA], gc_ref[...])
    r = _rms(ck[:, KV_LORA:], gr_ref[...]) * rot_ref[...]
    kpe_ref[...] = (r + pltpu.roll(r, QK_ROPE, axis=1))[:, :QK_ROPE]


def _latent(x, g, w, g_ckv, g_kpe, rot, tb):
    n, d = x.shape
    return pl.pallas_call(
        _latent_body,
        grid=(n // tb,),
        in_specs=[
            pl.BlockSpec((tb, d), lambda i: (i, 0)),
            _full((1, d)), _full((d, KV_LORA + LANES)), _full((1, KV_LORA)), _full((1, LANES)),
            pl.BlockSpec((tb, LANES), lambda i: (i, 0)),
        ],
        out_specs=[
            pl.BlockSpec((tb, KV_LORA), lambda i: (i, 0)),
            pl.BlockSpec((tb, QK_ROPE), lambda i: (i, 0)),
        ],
        out_shape=[jax.ShapeDtypeStruct((n, KV_LORA), F32), jax.ShapeDtypeStruct((n, QK_ROPE), F32)],
        compiler_params=_params(("arbitrary",)),
        name="latent",
    )(x, g, w, g_ckv, g_kpe, rot)


def _expand_body(c_ref, kpet_ref, wkt_ref, wv_ref, gk_ref, kt_ref, v_ref, *, tb):
    cb = c_ref[...].astype(BF16)
    v_ref[...] = jnp.dot(cb, wv_ref[...], preferred_element_type=F32).astype(BF16)
    kt = lax.dot_general(wkt_ref[...], cb, (((1,), (1,)), ((), ())), preferred_element_type=F32)
    kpet = kpet_ref[...].astype(BF16)
    for h in range(N_HEADS):
        k = kt[h * QK_NOPE:(h + 1) * QK_NOPE, :]
        ms = jnp.mean(k * k, axis=0, keepdims=True)
        kt_ref[h, 0:QK_NOPE, :] = (k * lax.rsqrt(ms + EPS) * gk_ref[...]).astype(BF16)
        kt_ref[h, QK_NOPE:QK_NOPE + QK_ROPE, :] = kpet
        kt_ref[h, QK_NOPE + QK_ROPE:, :] = jnp.zeros((QK_PAD - QK_NOPE - QK_ROPE, tb), BF16)


def _expand(c, kpet, wkt, wv, gk_t, tb):
    n = c.shape[0]
    return pl.pallas_call(
        functools.partial(_expand_body, tb=tb),
        grid=(n // tb,),
        in_specs=[
            pl.BlockSpec((tb, KV_LORA), lambda i: (i, 0)),
            pl.BlockSpec((QK_ROPE, tb), lambda i: (0, i)),
            _full((N_HEADS * QK_NOPE, KV_LORA)), _full((KV_LORA, N_HEADS * V_DIM)), _full((QK_NOPE, tb)),
        ],
        out_specs=[
            pl.BlockSpec((N_HEADS, QK_PAD, tb), lambda i: (0, 0, i)),
            pl.BlockSpec((tb, N_HEADS * V_DIM), lambda i: (i, 0)),
        ],
        out_shape=[jax.ShapeDtypeStruct((N_HEADS, QK_PAD, n), BF16),
                   jax.ShapeDtypeStruct((n, N_HEADS * V_DIM), BF16)],
        compiler_params=_params(("arbitrary",)),
        name="expand_latent",
    )(c, kpet, wkt, wv, gk_t)


def _query_body(x_ref, g_ref, wdq_ref, gqa_ref, wuq_ref, gn_ref, gr_ref, rot_ref, q_ref):
    h = _rms(x_ref[...], g_ref[...]).astype(BF16)
    cq = _rms(jnp.dot(h, wdq_ref[...], preferred_element_type=F32), gqa_ref[...]).astype(BF16)
    q = jnp.dot(cq, wuq_ref[...], preferred_element_type=F32)
    rot = rot_ref[...]
    for hd in range(N_HEADS):
        base = hd * QK_PAD
        qn = _rms(q[:, base:base + QK_NOPE], gn_ref[...])
        r = _rms(q[:, base + QK_NOPE:base + QK_PAD], gr_ref[...]) * rot
        qr = r + pltpu.roll(r, QK_ROPE, axis=1)
        q_ref[hd, :, 0:QK_NOPE] = (qn * ATTN_SCALE).astype(BF16)
        q_ref[hd, :, QK_NOPE:] = (qr * ATTN_SCALE).astype(BF16)


def _queries(x, g, w_dq, g_qa, w_uq, g_qn, g_qr, rot, tb):
    n, d = x.shape
    return pl.pallas_call(
        _query_body,
        grid=(n // tb,),
        in_specs=[
            pl.BlockSpec((tb, d), lambda i: (i, 0)),
            _full((1, d)), _full((d, Q_LORA)), _full((1, Q_LORA)), _full((Q_LORA, N_HEADS * QK_PAD)),
            _full((1, QK_NOPE)), _full((1, LANES)),
            pl.BlockSpec((tb, LANES), lambda i: (i, 0)),
        ],
        out_specs=pl.BlockSpec((N_HEADS, tb, QK_PAD), lambda i: (0, i, 0)),
        out_shape=jax.ShapeDtypeStruct((N_HEADS, n, QK_PAD), BF16),
        compiler_params=_params(("arbitrary",)),
        name="mla_queries",
    )(x, g, w_dq, g_qa, w_uq, g_qn, g_qr, rot)


def _attn_body(q_ref, kt_ref, v_ref, x_ref, wo_ref, o_ref, m_ref, l_ref, acc_ref, *, tq, tk, causal, tail_valid):
    i = pl.program_id(0)
    j = pl.program_id(1)
    last = i if causal else pl.num_programs(1) - 1

    @pl.when(j == 0)
    def _():
        m_ref[...] = jnp.full(m_ref.shape, NEG_INF, F32)
        l_ref[...] = jnp.zeros(l_ref.shape, F32)
        acc_ref[...] = jnp.zeros(acc_ref.shape, F32)

    def step(masked):
        if masked:
            kk = lax.broadcasted_iota(jnp.int32, (tq, tk), 1)
            if causal:
                qq = lax.broadcasted_iota(jnp.int32, (tq, tk), 0)
                keep = (kk // CHUNK) <= (qq // CHUNK)
            else:
                keep = kk < tail_valid
        for h in range(N_HEADS):
            s = jnp.dot(q_ref[h], kt_ref[h], preferred_element_type=F32)
            if masked:
                s = jnp.where(keep, s, NEG_INF)
            m_prev = m_ref[h]
            m_new = jnp.maximum(m_prev, jnp.max(s, axis=-1, keepdims=True))
            alpha = jnp.exp(m_prev - m_new)
            p = jnp.exp(s - m_new)
            l_ref[h] = alpha * l_ref[h] + jnp.sum(p, axis=-1, keepdims=True)
            cols = slice(h * V_DIM, (h + 1) * V_DIM)
            pv = jnp.dot(p.astype(BF16), v_ref[:, cols], preferred_element_type=F32)
            acc_ref[:, cols] = alpha * acc_ref[:, cols] + pv
            m_ref[h] = m_new

    @pl.when(j < last)
    def _():
        step(False)

    @pl.when(j == last)
    def _():
        step(True)
        for h in range(N_HEADS):
            cols = slice(h * V_DIM, (h + 1) * V_DIM)
            acc_ref[:, cols] = acc_ref[:, cols] / l_ref[h]
        o = acc_ref[...].astype(BF16)
        o_ref[...] = x_ref[...] + jnp.dot(o, wo_ref[...], preferred_element_type=F32)


def _attention(q, kt, v, x, w_o, *, tq, tk, causal, kv_tiles, tail_valid=0):
    n, d = x.shape
    if causal:
        kv_idx = lambda i, j: jnp.minimum(i, j)
    else:
        kv_idx = lambda i, j: i * kv_tiles + j
    return pl.pallas_call(
        functools.partial(_attn_body, tq=tq, tk=tk, causal=causal, tail_valid=tail_valid),
        grid=(n // tq, kv_tiles),
        in_specs=[
            pl.BlockSpec((N_HEADS, tq, QK_PAD), lambda i, j: (0, i, 0)),
            pl.BlockSpec((N_HEADS, QK_PAD, tk), lambda i, j: (0, 0, kv_idx(i, j))),
            pl.BlockSpec((tk, N_HEADS * V_DIM), lambda i, j: (kv_idx(i, j), 0)),
            pl.BlockSpec((tq, d), lambda i, j: (i, 0)),
            _full((N_HEADS * V_DIM, d)),
        ],
        out_specs=pl.BlockSpec((tq, d), lambda i, j: (i, 0)),
        out_shape=jax.ShapeDtypeStruct((n, d), F32),
        scratch_shapes=[
            pltpu.VMEM((N_HEADS, tq, 1), F32),
            pltpu.VMEM((N_HEADS, tq, 1), F32),
            pltpu.VMEM((tq, N_HEADS * V_DIM), F32),
        ],
        compiler_params=_params(("arbitrary", "arbitrary")),
        name="attention_causal" if causal else "attention_cached",
    )(q, kt, v, x, w_o)


def _extract_top(cur, rounds):
    rows = []
    for _ in range(rounds):
        m = jnp.max(cur, axis=0, keepdims=True)
        rows.append(m)
        cur = jnp.where(cur == m, NEG_INF, cur)
    return rows


def _peer_body(x_ref, g_ref, wqt_ref, keys_ref, u_ref, vt_ref, o_ref,
               xnt_ref, s_ref, top_ref, eb_ref, c_ref, tau_ref, act0_ref, act1_ref, w0_ref, w1_ref, acc_ref,
               *, tb, eb_rows):
    e = pl.program_id(1)
    nch = tb // LANES
    nblk = N_EXPERTS // eb_rows
    rpp = eb_rows // nch
    sub = 32

    @pl.when(e == 0)
    def _prologue():
        xn = _rms(x_ref[...], g_ref[...])
        xnt_ref[:, :tb] = xn.T.astype(BF16)
        qt = jnp.dot(wqt_ref[...], xnt_ref[:, :tb], preferred_element_type=F32)
        for hp in range(2 * PEER_HEADS):
            qs = qt[hp * N_KEYS:(hp + 1) * N_KEYS, :].astype(BF16)
            s_ref[hp, :, :tb] = jnp.dot(keys_ref[hp], qs, preferred_element_type=F32)

        def top_side(idx, carry):
            hp = idx // nch
            lanes = pl.ds(pl.multiple_of((idx % nch) * LANES, LANES), LANES)
            rows = _extract_top(s_ref[hp, :, lanes], PEER_TOPK)
            for r in range(PEER_TOPK):
                top_ref[hp, r:r + 1, lanes] = rows[r]
            return carry

        lax.fori_loop(0, 2 * PEER_HEADS * nch, top_side, 0)

        def top_pair(idx, carry):
            h = idx // nch
            lanes = pl.ds(pl.multiple_of((idx % nch) * LANES, LANES), LANES)
            a = top_ref[2 * h, :, lanes]
            b = top_ref[2 * h + 1, :, lanes]
            ri = lax.broadcasted_iota(jnp.int32, (PEER_TOPK, LANES), 0)
            pieces = []
            for k in range(4):
                pieces.append(jnp.where(ri < PEER_TOPK // (k + 1), a[k:k + 1, :] + b, NEG_INF))
            for l in range(3):
                ok = jnp.where(ri >= 4, ri, PEER_TOPK) < PEER_TOPK // (l + 1)
                pieces.append(jnp.where(ok, b[l:l + 1, :] + a, NEG_INF))
            cmax = a[0:1, :] + b[0:1, :]
            best = _extract_top(jnp.concatenate(pieces, axis=0), PEER_TOPK)
            z = jnp.zeros((1, LANES), F32)
            for m in best:
                z = z + jnp.exp(m - cmax)
            tau_ref[h, :, lanes] = best[-1]
            eb_ref[h, :, lanes] = jnp.exp(s_ref[2 * h + 1, :, lanes] - b[0:1, :])
            c_ref[h, :, lanes] = jnp.exp(s_ref[2 * h, :, lanes] - a[0:1, :]) / z
            return carry

        lax.fori_loop(0, PEER_HEADS * nch, top_pair, 0)
        acc_ref[...] = jnp.zeros(acc_ref.shape, F32)

    def piece(c, parity, do_a, do_b, do_c):
        act_w, act_r = (act0_ref, act1_ref) if parity == 0 else (act1_ref, act0_ref)
        w_r, w_w = (w0_ref, w1_ref) if parity == 0 else (w1_ref, w0_ref)
        rows = pl.ds(pl.multiple_of(c * rpp, rpp), rpp)
        anchors = {}
        if do_a:
            res = jnp.dot(u_ref[rows, :], xnt_ref[:, :tb], preferred_element_type=F32)
            act_w[rows, :tb] = res
            anchors[N_KEYS // sub // 2] = jnp.minimum(jnp.abs(res[rpp - 1:rpp, tb - LANES:tb]), 0.0)
        if do_c:
            upd = acc_ref[rows, :tb] + jnp.dot(vt_ref[rows, :], w_r[:, :tb], preferred_element_type=F32)
            acc_ref[rows, :tb] = upd
            anchors[3 * N_KEYS // sub // 4] = jnp.minimum(jnp.abs(upd[rpp - 1:rpp, tb - LANES:tb]), 0.0)
        if do_b:
            lanes = pl.ds(pl.multiple_of(c * LANES, LANES), LANES)
            first = pl.ds(pl.multiple_of((e - 1) * 8, 8), 8)
            tau = [tau_ref[h, :, lanes] for h in range(PEER_HEADS)]
            for jb in range(N_KEYS // sub):
                if jb in anchors:
                    tau = [t + anchors[jb] for t in tau]
                rows_j = slice(jb * sub, (jb + 1) * sub)
                gsum = [jnp.zeros((sub, LANES), F32) for _ in range(8)]
                for h in range(PEER_HEADS):
                    sb = s_ref[2 * h + 1, rows_j, lanes]
                    eb = eb_ref[h, rows_j, lanes]
                    sa = s_ref[2 * h, first, lanes]
                    cf = c_ref[h, first, lanes]
                    for r in range(8):
                        hit = (sb + sa[r:r + 1, :]) >= tau[h]
                        gsum[r] = gsum[r] + jnp.where(hit, eb * cf[r:r + 1, :], 0.0)
                for r in range(8):
                    erow = slice(r * N_KEYS + jb * sub, r * N_KEYS + (jb + 1) * sub)
                    a = act_r[erow, lanes]
                    gelu = 0.5 * a * (1.0 + lax.erf(a * math.sqrt(0.5)))
                    w_w[erow, lanes] = (gelu * gsum[r]).astype(BF16)

    def run(parity, do_a, do_b, do_c):
        def body(c, carry):
            piece(c, parity, do_a, do_b, do_c)
            return carry
        lax.fori_loop(0, nch, body, 0)

    pl.when(e == 0)(lambda: run(0, True, False, False))
    pl.when(e == 1)(lambda: run(1, True, True, False))
    for parity in range(2):
        pl.when((e >= 2) & (e < nblk) & (e % 2 == parity))(lambda parity=parity: run(parity, True, True, True))
    pl.when(e == nblk)(lambda: run(nblk % 2, False, True, True))

    @pl.when(e == nblk + 1)
    def _():
        run((nblk + 1) % 2, False, False, True)
        o_ref[...] = x_ref[...] + acc_ref[:, :tb].T


def _peer(x, g, wqt, keys, u, vt, *, tb):
    n, d = x.shape
    eb_rows = 8 * N_KEYS
    nblk = N_EXPERTS // eb_rows
    tbp = tb + LANES
    return pl.pallas_call(
        functools.partial(_peer_body, tb=tb, eb_rows=eb_rows),
        grid=(n // tb, nblk + 2),
        in_specs=[
            pl.BlockSpec((tb, d), lambda i, e: (i, 0)),
            _full((1, d)), _full((PEER_HEADS * D_KEY, d)), _full((2 * PEER_HEADS, N_KEYS, D_KEY // 2)),
            pl.BlockSpec((eb_rows, d), lambda i, e: (jnp.minimum(e, nblk - 1), 0)),
            pl.BlockSpec((d, eb_rows), lambda i, e: (0, jnp.clip(e - 2, 0, nblk - 1))),
        ],
        out_specs=pl.BlockSpec((tb, d), lambda i, e: (i, 0)),
        out_shape=jax.ShapeDtypeStruct((n, d), F32),
        scratch_shapes=[
            pltpu.VMEM((d, tbp), BF16),
            pltpu.VMEM((2 * PEER_HEADS, N_KEYS, tbp), F32),
            pltpu.VMEM((2 * PEER_HEADS, PEER_TOPK, tb), F32),
            pltpu.VMEM((PEER_HEADS, N_KEYS, tbp), F32),
            pltpu.VMEM((PEER_HEADS, N_KEYS, tbp), F32),
            pltpu.VMEM((PEER_HEADS, 1, tb), F32),
            pltpu.VMEM((eb_rows, tbp), F32),
            pltpu.VMEM((eb_rows, tbp), F32),
            pltpu.VMEM((eb_rows, tbp), BF16),
            pltpu.VMEM((eb_rows, tbp), BF16),
            pltpu.VMEM((d, tbp), F32),
        ],
        compiler_params=_params(("arbitrary", "arbitrary")),
        name="peer",
    )(x, g, wqt, keys, u, vt)


def _rot_table(pos):
    inv = ROPE_BASE ** (-jnp.arange(0, QK_ROPE, 2, dtype=F32) / QK_ROPE)
    ang = pos.astype(F32)[:, None] * inv[None, :]
    cos, sin = jnp.cos(ang), jnp.sin(ang)
    return jnp.concatenate([cos, cos, -sin, sin], axis=-1)


def _dup_rope(w):
    h = QK_ROPE // 2
    return jnp.concatenate([w[..., :h], w[..., h:], w[..., h:], w[..., :h]], axis=-1)


def kernel(x_prompt, x_sample, state_conv, cache_ckv, cache_kpe, p_prompt, p_sample, norm_mix, norm_ffn, conv_w_in, conv_kernel, conv_w_out, kv_norm_in, w_dkv_kr, ckv_norm, kpe_norm, w_ukv, knope_norm, w_dq, q_a_norm, w_uq, qnope_norm, qpe_norm, w_o, peer_w_query, peer_sub_keys, peer_u, peer_v, ple_norm, ple_w_gate, ple_w_proj):
    d = D_MODEL
    row = lambda a: a.reshape(1, -1)

    w_in = conv_w_in[0].astype(BF16)
    w_out = conv_w_out[0].astype(BF16)
    w_lat = jnp.concatenate([w_dkv_kr[:, :KV_LORA], _dup_rope(w_dkv_kr[:, KV_LORA:])], axis=1).astype(BF16)
    g_kpe = row(_dup_rope(kpe_norm))
    ukv = w_ukv.reshape(KV_LORA, N_HEADS, QK_NOPE + V_DIM)
    wkt = ukv[:, :, :QK_NOPE].reshape(KV_LORA, N_HEADS * QK_NOPE).T.astype(BF16)
    wv = ukv[:, :, QK_NOPE:].reshape(KV_LORA, N_HEADS * V_DIM).astype(BF16)
    uq = w_uq[0].reshape(Q_LORA, N_HEADS, QK_NOPE + QK_ROPE)
    w_uq_p = jnp.concatenate([uq[:, :, :QK_NOPE], _dup_rope(uq[:, :, QK_NOPE:])], axis=-1)
    w_uq_p = w_uq_p.reshape(Q_LORA, N_HEADS * QK_PAD).astype(BF16)
    g_qpe = row(_dup_rope(qpe_norm[0]))
    w_dq_b = w_dq[0].astype(BF16)
    w_o_b = w_o[0].astype(BF16)
    peer_w = []
    for i in range(2):
        peer_w.append((
            row(norm_ffn[i]),
            peer_w_query[i].T.astype(BF16),
            peer_sub_keys[i].reshape(2 * PEER_HEADS, N_KEYS, D_KEY // 2).astype(BF16),
            peer_u[i].astype(BF16),
            peer_v[i].T.astype(BF16),
        ))
    ple_w = [(row(ple_norm[i]), ple_w_gate[i].astype(BF16), ple_w_proj[i].astype(BF16)) for i in range(2)]

    def run_group(x, p, pos, conv_prev, past_ckv, past_kpe):
        b, t = x.shape[:2]
        n = b * t
        tb = min(512, n)
        x1, conv_state = _conv_mixer(x, conv_prev, row(norm_mix[0]), w_in, conv_kernel[0], w_out, min(512, t))
        xf = x1.reshape(n, d)
        xf = _peer(xf, *peer_w[0], tb=tb)
        xf = _ple(xf, p[0].reshape(n, D_PLE), *ple_w[0], tb)
        rot = _rot_table(pos)
        ckv, kpe = _latent(xf, row(kv_norm_in), w_lat, row(ckv_norm), g_kpe, rot, tb)
        q = _queries(xf, row(norm_mix[1]), w_dq_b, row(q_a_norm[0]), w_uq_p, row(qnope_norm[0]), g_qpe, rot, tb)
        tk = 512
        gk_t = jnp.broadcast_to(knope_norm[:, None], (QK_NOPE, tk))
        if past_ckv is None:
            kt, v = _expand(ckv, kpe.T, wkt, wv, gk_t, tk)
            xf = _attention(q, kt, v, xf, w_o_b, tq=tk, tk=tk, causal=True, kv_tiles=n // tk)
        else:
            total = past_ckv.shape[1] + t
            tiles = -(-total // tk)
            pad = tiles * tk - total
            c_all = jnp.concatenate([past_ckv, ckv.reshape(b, t, KV_LORA), jnp.zeros((b, pad, KV_LORA), F32)], axis=1)
            r_all = jnp.concatenate([past_kpe, kpe.reshape(b, t, QK_ROPE), jnp.zeros((b, pad, QK_ROPE), F32)], axis=1)
            kt, v = _expand(c_all.reshape(b * tiles * tk, KV_LORA), r_all.reshape(b * tiles * tk, QK_ROPE).T, wkt, wv, gk_t, tk)
            xf = _attention(q, kt, v, xf, w_o_b, tq=t, tk=tk, causal=False, kv_tiles=tiles, tail_valid=tk - pad)
        xf = _peer(xf, *peer_w[1], tb=tb)
        xf = _ple(xf, p[1].reshape(n, D_PLE), *ple_w[1], tb)
        return (xf.reshape(b, t, d), conv_state[None], ckv.reshape(b, t, KV_LORA), kpe.reshape(b, t, QK_ROPE))

    pos_p = jnp.arange(x_prompt.shape[1])
    pos_s = jnp.tile(cache_ckv.shape[1] + jnp.arange(x_sample.shape[1]), x_sample.shape[0])
    zero_conv = jnp.zeros((x_prompt.shape[0], 2, d), F32)
    y_p, conv_p, ckv_p, kpe_p = run_group(x_prompt, p_prompt, pos_p, zero_conv, None, None)
    y_s, conv_s, ckv_s, kpe_s = run_group(x_sample, p_sample, pos_s, state_conv[0], cache_ckv, cache_kpe)
    return (y_p, y_s, conv_p, ckv_p, kpe_p, conv_s, ckv_s, kpe_s)
```

```python
import functools
import math

import jax
import jax.numpy as jnp
from jax import lax
from jax.experimental import pallas as pl
from jax.experimental.pallas import tpu as pltpu

D_MODEL = 1024
CHUNK = 64
N_HEADS = 8
QK_NOPE = 128
QK_ROPE = 64
V_DIM = 128
Q_LORA = 512
KV_LORA = 256
ROPE_BASE = 10000.0
ATTN_SCALE = (QK_NOPE + QK_ROPE) ** -0.5
PEER_HEADS = 8
N_KEYS = 128
N_EXPERTS = N_KEYS * N_KEYS
D_KEY = 256
PEER_TOPK = 16
D_PLE = 256
EPS = 1e-6

LANES = 128
QK_PAD = 256
V_PAD = 144
Q_SCALE = ATTN_SCALE * math.log2(math.e)
VMEM_LIMIT = 56 * 1024 * 1024

F32 = jnp.float32
BF16 = jnp.bfloat16
NEG_INF = float("-inf")


def _params(sem, flags=None):
    return pltpu.CompilerParams(dimension_semantics=sem, vmem_limit_bytes=VMEM_LIMIT, flags=flags)


def _rms(x, g):
    return x * lax.rsqrt(jnp.mean(x * x, axis=-1, keepdims=True) + EPS) * g


def _full(shape):
    nd = len(shape)
    return pl.BlockSpec(shape, lambda *_: (0,) * nd)


def _conv_body(x_ref, prev_ref, g_ref, win_ref, ck_ref, wout_ref, y_ref, st_ref, carry_ref, *, tb):
    @pl.when(pl.program_id(1) == 0)
    def _():
        carry_ref[...] = prev_ref[0]

    x = x_ref[0]
    h = _rms(x, g_ref[...]).astype(BF16)
    bcv = jnp.dot(h, win_ref[...], preferred_element_type=F32)
    b_gate = bcv[:, :D_MODEL]
    u = bcv[:, D_MODEL:2 * D_MODEL] * bcv[:, 2 * D_MODEL:]
    row = lax.broadcasted_iota(jnp.int32, (tb, D_MODEL), 0)
    p0 = carry_ref[0:1, :]
    p1 = carry_ref[1:2, :]
    u1 = jnp.where(row == 0, p1, pltpu.roll(u, 1, axis=0))
    u2 = jnp.where(row == 0, p0, jnp.where(row == 1, p1, pltpu.roll(u, 2, axis=0)))
    conv = ck_ref[0:1, :] * u2 + ck_ref[1:2, :] * u1 + ck_ref[2:3, :] * u
    y = jnp.dot((b_gate * conv).astype(BF16), wout_ref[...], preferred_element_type=F32)
    y_ref[0] = x + y
    last = u[tb - 2:tb, :]
    carry_ref[...] = last
    st_ref[0] = last


def _conv_mixer(x, prev, g, w_in, ck, w_out, tb):
    b, t, d = x.shape
    return pl.pallas_call(
        functools.partial(_conv_body, tb=tb),
        grid=(b, t // tb),
        in_specs=[
            pl.BlockSpec((1, tb, d), lambda i, j: (i, j, 0)),
            pl.BlockSpec((1, 2, d), lambda i, j: (i, 0, 0)),
            _full((1, d)), _full((d, 3 * d)), _full((3, d)), _full((d, d)),
        ],
        out_specs=[
            pl.BlockSpec((1, tb, d), lambda i, j: (i, j, 0)),
            pl.BlockSpec((1, 2, d), lambda i, j: (i, 0, 0)),
        ],
        out_shape=[jax.ShapeDtypeStruct((b, t, d), F32), jax.ShapeDtypeStruct((b, 2, d), F32)],
        scratch_shapes=[pltpu.VMEM((2, d), F32)],
        compiler_params=_params(("arbitrary", "arbitrary")),
        name="conv_mixer",
    )(x, prev, g, w_in, ck, w_out)


def _ple_body(x_ref, p_ref, g_ref, wg_ref, wp_ref, o_ref):
    x = x_ref[...]
    hn = _rms(x, g_ref[...]).astype(BF16)
    gate = jax.nn.sigmoid(jnp.dot(hn, wg_ref[...], preferred_element_type=F32))
    proj = jnp.dot(p_ref[...].astype(BF16), wp_ref[...], preferred_element_type=F32)
    o_ref[...] = x + proj * gate


def _ple(x, p, g, w_gate, w_proj, tb):
    n, d = x.shape
    return pl.pallas_call(
        _ple_body,
        grid=(n // tb,),
        in_specs=[
            pl.BlockSpec((tb, d), lambda i: (i, 0)),
            pl.BlockSpec((tb, D_PLE), lambda i: (i, 0)),
            _full((1, d)), _full((d, d)), _full((D_PLE, d)),
        ],
        out_specs=pl.BlockSpec((tb, d), lambda i: (i, 0)),
        out_shape=jax.ShapeDtypeStruct((n, d), F32),
        compiler_params=_params(("arbitrary",)),
        name="ple",
    )(x, p, g, w_gate, w_proj)


def _latent_body(x_ref, g_ref, w_ref, gc_ref, gr_ref, rot_ref, ckv_ref, kpe_ref):
    h = _rms(x_ref[...], g_ref[...]).astype(BF16)
    ck = jnp.dot(h, w_ref[...], preferred_element_type=F32)
    ckv_ref[...] = _rms(ck[:, :KV_LORA], gc_ref[...])
    r = _rms(ck[:, KV_LORA:], gr_ref[...]) * rot_ref[...]
    kpe_ref[...] = (r + pltpu.roll(r, QK_ROPE, axis=1))[:, :QK_ROPE]


def _latent(x, g, w, g_ckv, g_kpe, rot, tb):
    n, d = x.shape
    return pl.pallas_call(
        _latent_body,
        grid=(n // tb,),
        in_specs=[
            pl.BlockSpec((tb, d), lambda i: (i, 0)),
            _full((1, d)), _full((d, KV_LORA + LANES)), _full((1, KV_LORA)), _full((1, LANES)),
            pl.BlockSpec((tb, LANES), lambda i: (i, 0)),
        ],
        out_specs=[
            pl.BlockSpec((tb, KV_LORA), lambda i: (i, 0)),
            pl.BlockSpec((tb, QK_ROPE), lambda i: (i, 0)),
        ],
        out_shape=[jax.ShapeDtypeStruct((n, KV_LORA), F32), jax.ShapeDtypeStruct((n, QK_ROPE), F32)],
        compiler_params=_params(("arbitrary",)),
        name="latent",
    )(x, g, w, g_ckv, g_kpe, rot)


def _expand_body(c_ref, kpe_ref, wk_ref, wvt_ref, gk_ref, k_ref, vt_ref, *, tb):
    cb = c_ref[...].astype(BF16)
    kn = jnp.dot(cb, wk_ref[...], preferred_element_type=F32)
    vt = lax.dot_general(wvt_ref[...], cb, (((1,), (1,)), ((), ())), preferred_element_type=F32)
    kpe = kpe_ref[...].astype(BF16)
    for h in range(N_HEADS):
        k_ref[h, :, 0:QK_NOPE] = _rms(kn[:, h * QK_NOPE:(h + 1) * QK_NOPE], gk_ref[...]).astype(BF16)
        k_ref[h, :, QK_NOPE:QK_NOPE + QK_ROPE] = kpe
        k_ref[h, :, QK_NOPE + QK_ROPE:] = jnp.zeros((tb, QK_PAD - QK_NOPE - QK_ROPE), BF16)
        vt_ref[h, 0:V_DIM, :] = vt[h * V_DIM:(h + 1) * V_DIM, :].astype(BF16)
        vt_ref[h, V_DIM:, :] = jnp.ones((V_PAD - V_DIM, tb), BF16)


def _expand(c, kpe, wk, wvt, gk, tb):
    n = c.shape[0]
    return pl.pallas_call(
        functools.partial(_expand_body, tb=tb),
        grid=(n // tb,),
        in_specs=[
            pl.BlockSpec((tb, KV_LORA), lambda i: (i, 0)),
            pl.BlockSpec((tb, QK_ROPE), lambda i: (i, 0)),
            _full((KV_LORA, N_HEADS * QK_NOPE)), _full((N_HEADS * V_DIM, KV_LORA)), _full((1, QK_NOPE)),
        ],
        out_specs=[
            pl.BlockSpec((N_HEADS, tb, QK_PAD), lambda i: (0, i, 0)),
            pl.BlockSpec((N_HEADS, V_PAD, tb), lambda i: (0, 0, i)),
        ],
        out_shape=[jax.ShapeDtypeStruct((N_HEADS, n, QK_PAD), BF16),
                   jax.ShapeDtypeStruct((N_HEADS, V_PAD, n), BF16)],
        compiler_params=_params(("arbitrary",)),
        name="expand_latent",
    )(c, kpe, wk, wvt, gk)


def _query_body(x_ref, g_ref, wdq_ref, gqa_ref, wuqt_ref, gn_ref, gr_ref, cos_ref, sin_ref, qt_ref, *, tb):
    h = _rms(x_ref[...], g_ref[...]).astype(BF16)
    cq = _rms(jnp.dot(h, wdq_ref[...], preferred_element_type=F32), gqa_ref[...]).astype(BF16)
    qt = lax.dot_general(wuqt_ref[...], cq, (((1,), (1,)), ((), ())), preferred_element_type=F32)
    cos = cos_ref[...]
    sin = sin_ref[...]
    half = QK_ROPE // 2
    for hd in range(N_HEADS):
        base = hd * (QK_NOPE + QK_ROPE)
        qn = qt[base:base + QK_NOPE, :]
        qn = qn * lax.rsqrt(jnp.mean(qn * qn, axis=0, keepdims=True) + EPS) * gn_ref[...]
        x1 = qt[base + QK_NOPE:base + QK_NOPE + half, :]
        x2 = qt[base + QK_NOPE + half:base + QK_NOPE + QK_ROPE, :]
        ms = (jnp.sum(x1 * x1, axis=0, keepdims=True) + jnp.sum(x2 * x2, axis=0, keepdims=True)) / QK_ROPE
        rs = lax.rsqrt(ms + EPS)
        n1 = x1 * rs * gr_ref[0:half, :]
        n2 = x2 * rs * gr_ref[half:QK_ROPE, :]
        qt_ref[hd, 0:QK_NOPE, :] = (qn * Q_SCALE).astype(BF16)
        qt_ref[hd, QK_NOPE:QK_NOPE + half, :] = ((n1 * cos - n2 * sin) * Q_SCALE).astype(BF16)
        qt_ref[hd, QK_NOPE + half:QK_NOPE + QK_ROPE, :] = ((n2 * cos + n1 * sin) * Q_SCALE).astype(BF16)
        qt_ref[hd, QK_NOPE + QK_ROPE:, :] = jnp.zeros((QK_PAD - QK_NOPE - QK_ROPE, tb), BF16)


def _queries(x, g, w_dq, g_qa, w_uqt, g_qn, g_qr, cos_t, sin_t, tb):
    n, d = x.shape
    return pl.pallas_call(
        functools.partial(_query_body, tb=tb),
        grid=(n // tb,),
        in_specs=[
            pl.BlockSpec((tb, d), lambda i: (i, 0)),
            _full((1, d)), _full((d, Q_LORA)), _full((1, Q_LORA)),
            _full((N_HEADS * (QK_NOPE + QK_ROPE), Q_LORA)),
            _full((QK_NOPE, tb)), _full((QK_ROPE, tb)),
            pl.BlockSpec((QK_ROPE // 2, tb), lambda i: (0, i)),
            pl.BlockSpec((QK_ROPE // 2, tb), lambda i: (0, i)),
        ],
        out_specs=pl.BlockSpec((N_HEADS, QK_PAD, tb), lambda i: (0, 0, i)),
        out_shape=jax.ShapeDtypeStruct((N_HEADS, QK_PAD, n), BF16),
        compiler_params=_params(("arbitrary",)),
        name="mla_queries",
    )(x, g, w_dq, g_qa, w_uqt, g_qn, g_qr, cos_t, sin_t)


def _attn_body(qt_ref, k_ref, vt_ref, x_ref, wot_ref, o_ref, m_ref, acc_ref, s_ref, p_ref, ot_ref,
               *, tq, tk, rows_out, causal, tail_valid):
    i = pl.program_id(0)
    j = pl.program_id(1)
    last = i if causal else pl.num_programs(1) - 1

    @pl.when(j == 0)
    def _():
        m_ref[...] = jnp.full(m_ref.shape, NEG_INF, F32)
        acc_ref[...] = jnp.zeros(acc_ref.shape, F32)

    def step(masked):
        if masked:
            kk = lax.broadcasted_iota(jnp.int32, (tk, tq), 0)
            if causal:
                qq = lax.broadcasted_iota(jnp.int32, (tk, tq), 1)
                keep = (kk // CHUNK) <= (qq // CHUNK)
            else:
                keep = kk < tail_valid
        for h in range(N_HEADS):
            sb = s_ref.at[h % 2]
            pb = p_ref.at[h % 2]
            sb[:, :tq] = jnp.dot(k_ref[h], qt_ref[h], preferred_element_type=F32)
            s = sb[:, :tq]
            if masked:
                s = jnp.where(keep, s, NEG_INF)
            m_prev = m_ref[h]
            m_new = jnp.maximum(m_prev, jnp.max(s, axis=0, keepdims=True))
            alpha = jnp.exp2(m_prev - m_new)
            pb[:, :tq] = jnp.exp2(s - m_new).astype(BF16)
            pv = jnp.dot(vt_ref[h], pb[:, :tq], preferred_element_type=F32)
            acc_ref[h] = acc_ref[h] * alpha + pv
            m_ref[h] = m_new

    @pl.when(j < last)
    def _():
        step(False)

    @pl.when(j == last)
    def _():
        step(True)
        for h in range(N_HEADS):
            a = acc_ref[h]
            ot_ref[h * V_DIM:(h + 1) * V_DIM, :] = (a[0:V_DIM, :] * (1.0 / a[V_DIM:V_DIM + 1, :])).astype(BF16)
        out_t = jnp.dot(wot_ref[...], ot_ref[...], preferred_element_type=F32)
        o_ref[...] = x_ref[...] + out_t.T[:rows_out, :]


def _attention(qt, k, vt, x, w_ot, *, tq, tk, rows_out, causal, kv_tiles, tail_valid=0):
    n, d = x.shape
    nq = qt.shape[2] // tq
    assert nq * rows_out == n
    if causal:
        kv_idx = lambda i, j: jnp.minimum(i, j)
    else:
        kv_idx = lambda i, j: i * kv_tiles + j
    tqp = tq + LANES
    return pl.pallas_call(
        functools.partial(_attn_body, tq=tq, tk=tk, rows_out=rows_out, causal=causal, tail_valid=tail_valid),
        grid=(nq, kv_tiles),
        in_specs=[
            pl.BlockSpec((N_HEADS, QK_PAD, tq), lambda i, j: (0, 0, i)),
            pl.BlockSpec((N_HEADS, tk, QK_PAD), lambda i, j: (0, kv_idx(i, j), 0)),
            pl.BlockSpec((N_HEADS, V_PAD, tk), lambda i, j: (0, 0, kv_idx(i, j))),
            pl.BlockSpec((rows_out, d), lambda i, j: (i, 0)),
            _full((d, N_HEADS * V_DIM)),
        ],
        out_specs=pl.BlockSpec((rows_out, d), lambda i, j: (i, 0)),
        out_shape=jax.ShapeDtypeStruct((n, d), F32),
        scratch_shapes=[
            pltpu.VMEM((N_HEADS, 1, tq), F32),
            pltpu.VMEM((N_HEADS, V_PAD, tq), F32),
            pltpu.VMEM((2, tk, tqp), F32),
            pltpu.VMEM((2, tk, tqp), BF16),
            pltpu.VMEM((N_HEADS * V_DIM, tq), BF16),
        ],
        compiler_params=_params(("arbitrary", "arbitrary")),
        name="attention_causal" if causal else "attention_cached",
    )(qt, k, vt, x, w_ot)


def _extract_top(cur, rounds):
    rows = []
    for _ in range(rounds):
        m = jnp.max(cur, axis=0, keepdims=True)
        rows.append(m)
        cur = jnp.where(cur == m, NEG_INF, cur)
    return rows


def _peer_body(x_ref, g_ref, wqt_ref, keys_ref, u_ref, vt_ref, o_ref,
               xnt_ref, s_ref, top_ref, eb_ref, c_ref, tau_ref, act0_ref, act1_ref, w0_ref, w1_ref, acc_ref,
               *, tb, eb_rows):
    e = pl.program_id(1)
    nch = tb // LANES
    nblk = N_EXPERTS // eb_rows
    rpp = eb_rows // nch
    sub = 32

    @pl.when(e == 0)
    def _prologue():
        xn = _rms(x_ref[...], g_ref[...])
        xnt_ref[:, :tb] = xn.T.astype(BF16)
        qt = jnp.dot(wqt_ref[...], xnt_ref[:, :tb], preferred_element_type=F32)
        for hp in range(2 * PEER_HEADS):
            qs = qt[hp * N_KEYS:(hp + 1) * N_KEYS, :].astype(BF16)
            s_ref[hp, :, :tb] = jnp.dot(keys_ref[hp], qs, preferred_element_type=F32)

        def top_side(idx, carry):
            hp = idx // nch
            lanes = pl.ds(pl.multiple_of((idx % nch) * LANES, LANES), LANES)
            rows = _extract_top(s_ref[hp, :, lanes], PEER_TOPK)
            for r in range(PEER_TOPK):
                top_ref[hp, r:r + 1, lanes] = rows[r]
            return carry

        lax.fori_loop(0, 2 * PEER_HEADS * nch, top_side, 0)

        def top_pair(idx, carry):
            h = idx // nch
            lanes = pl.ds(pl.multiple_of((idx % nch) * LANES, LANES), LANES)
            a = top_ref[2 * h, :, lanes]
            b = top_ref[2 * h + 1, :, lanes]
            ri = lax.broadcasted_iota(jnp.int32, (PEER_TOPK, LANES), 0)
            pieces = []
            for k in range(4):
                pieces.append(jnp.where(ri < PEER_TOPK // (k + 1), a[k:k + 1, :] + b, NEG_INF))
            for l in range(3):
                ok = jnp.where(ri >= 4, ri, PEER_TOPK) < PEER_TOPK // (l + 1)
                pieces.append(jnp.where(ok, b[l:l + 1, :] + a, NEG_INF))
            cmax = a[0:1, :] + b[0:1, :]
            best = _extract_top(jnp.concatenate(pieces, axis=0), PEER_TOPK)
            z = jnp.zeros((1, LANES), F32)
            for m in best:
                z = z + jnp.exp(m - cmax)
            tau_ref[h, :, lanes] = best[-1]
            eb_ref[h, :, lanes] = jnp.exp(s_ref[2 * h + 1, :, lanes] - b[0:1, :])
            c_ref[h, :, lanes] = jnp.exp(s_ref[2 * h, :, lanes] - a[0:1, :]) / z
            return carry

        lax.fori_loop(0, PEER_HEADS * nch, top_pair, 0)
        acc_ref[...] = jnp.zeros(acc_ref.shape, F32)

    def piece(c, parity, do_a, do_b, do_c):
        act_w, act_r = (act0_ref, act1_ref) if parity == 0 else (act1_ref, act0_ref)
        w_r, w_w = (w0_ref, w1_ref) if parity == 0 else (w1_ref, w0_ref)
        rows = pl.ds(pl.multiple_of(c * rpp, rpp), rpp)
        anchors = {}
        if do_a:
            res = jnp.dot(u_ref[rows, :], xnt_ref[:, :tb], preferred_element_type=F32)
            act_w[rows, :tb] = res
            anchors[N_KEYS // sub // 2] = jnp.minimum(jnp.abs(res[rpp - 1:rpp, tb - LANES:tb]), 0.0)
        if do_c:
            upd = acc_ref[rows, :tb] + jnp.dot(vt_ref[rows, :], w_r[:, :tb], preferred_element_type=F32)
            acc_ref[rows, :tb] = upd
            anchors[3 * N_KEYS // sub // 4] = jnp.minimum(jnp.abs(upd[rpp - 1:rpp, tb - LANES:tb]), 0.0)
        if do_b:
            lanes = pl.ds(pl.multiple_of(c * LANES, LANES), LANES)
            first = pl.ds(pl.multiple_of((e - 1) * 8, 8), 8)
            tau = [tau_ref[h, :, lanes] for h in range(PEER_HEADS)]
            for jb in range(N_KEYS // sub):
                if jb in anchors:
                    tau = [t + anchors[jb] for t in tau]
                rows_j = slice(jb * sub, (jb + 1) * sub)
                gsum = [jnp.zeros((sub, LANES), F32) for _ in range(8)]
                for h in range(PEER_HEADS):
                    sb = s_ref[2 * h + 1, rows_j, lanes]
                    eb = eb_ref[h, rows_j, lanes]
                    sa = s_ref[2 * h, first, lanes]
                    cf = c_ref[h, first, lanes]
                    for r in range(8):
                        hit = (sb + sa[r:r + 1, :]) >= tau[h]
                        gsum[r] = gsum[r] + jnp.where(hit, eb * cf[r:r + 1, :], 0.0)
                for r in range(8):
                    erow = slice(r * N_KEYS + jb * sub, r * N_KEYS + (jb + 1) * sub)
                    a = act_r[erow, lanes]
                    gelu = 0.5 * a * (1.0 + lax.erf(a * math.sqrt(0.5)))
                    w_w[erow, lanes] = (gelu * gsum[r]).astype(BF16)

    def run(parity, do_a, do_b, do_c):
        def body(c, carry):
            piece(c, parity, do_a, do_b, do_c)
            return carry
        lax.fori_loop(0, nch, body, 0)

    pl.when(e == 0)(lambda: run(0, True, False, False))
    pl.when(e == 1)(lambda: run(1, True, True, False))
    for parity in range(2):
        pl.when((e >= 2) & (e < nblk) & (e % 2 == parity))(lambda parity=parity: run(parity, True, True, True))
    pl.when(e == nblk)(lambda: run(nblk % 2, False, True, True))

    @pl.when(e == nblk + 1)
    def _():
        run((nblk + 1) % 2, False, False, True)
        o_ref[...] = x_ref[...] + acc_ref[:, :tb].T


def _peer(x, g, wqt, keys, u, vt, *, tb):
    n, d = x.shape
    eb_rows = 8 * N_KEYS
    nblk = N_EXPERTS // eb_rows
    tbp = tb + LANES
    return pl.pallas_call(
        functools.partial(_peer_body, tb=tb, eb_rows=eb_rows),
        grid=(n // tb, nblk + 2),
        in_specs=[
            pl.BlockSpec((tb, d), lambda i, e: (i, 0)),
            _full((1, d)), _full((PEER_HEADS * D_KEY, d)), _full((2 * PEER_HEADS, N_KEYS, D_KEY // 2)),
            pl.BlockSpec((eb_rows, d), lambda i, e: (jnp.minimum(e, nblk - 1), 0)),
            pl.BlockSpec((d, eb_rows), lambda i, e: (0, jnp.clip(e - 2, 0, nblk - 1))),
        ],
        out_specs=pl.BlockSpec((tb, d), lambda i, e: (i, 0)),
        out_shape=jax.ShapeDtypeStruct((n, d), F32),
        scratch_shapes=[
            pltpu.VMEM((d, tbp), BF16),
            pltpu.VMEM((2 * PEER_HEADS, N_KEYS, tbp), F32),
            pltpu.VMEM((2 * PEER_HEADS, PEER_TOPK, tb), F32),
            pltpu.VMEM((PEER_HEADS, N_KEYS, tbp), F32),
            pltpu.VMEM((PEER_HEADS, N_KEYS, tbp), F32),
            pltpu.VMEM((PEER_HEADS, 1, tb), F32),
            pltpu.VMEM((eb_rows, tbp), F32),
            pltpu.VMEM((eb_rows, tbp), F32),
            pltpu.VMEM((eb_rows, tbp), BF16),
            pltpu.VMEM((eb_rows, tbp), BF16),
            pltpu.VMEM((d, tbp), F32),
        ],
        compiler_params=_params(("arbitrary", "arbitrary")),
        name="peer",
    )(x, g, wqt, keys, u, vt)


def _rope_tables(pos):
    inv = ROPE_BASE ** (-jnp.arange(0, QK_ROPE, 2, dtype=F32) / QK_ROPE)
    ang = pos.astype(F32)[:, None] * inv[None, :]
    return jnp.cos(ang), jnp.sin(ang)


def _rot_table(pos):
    cos, sin = _rope_tables(pos)
    return jnp.concatenate([cos, cos, -sin, sin], axis=-1)


def _dup_rope(w):
    h = QK_ROPE // 2
    return jnp.concatenate([w[..., :h], w[..., h:], w[..., h:], w[..., :h]], axis=-1)


def kernel(x_prompt, x_sample, state_conv, cache_ckv, cache_kpe, p_prompt, p_sample, norm_mix, norm_ffn, conv_w_in, conv_kernel, conv_w_out, kv_norm_in, w_dkv_kr, ckv_norm, kpe_norm, w_ukv, knope_norm, w_dq, q_a_norm, w_uq, qnope_norm, qpe_norm, w_o, peer_w_query, peer_sub_keys, peer_u, peer_v, ple_norm, ple_w_gate, ple_w_proj):
    d = D_MODEL
    row = lambda a: a.reshape(1, -1)

    w_in = conv_w_in[0].astype(BF16)
    w_out = conv_w_out[0].astype(BF16)
    w_lat = jnp.concatenate([w_dkv_kr[:, :KV_LORA], _dup_rope(w_dkv_kr[:, KV_LORA:])], axis=1).astype(BF16)
    g_kpe = row(_dup_rope(kpe_norm))
    ukv = w_ukv.reshape(KV_LORA, N_HEADS, QK_NOPE + V_DIM)
    wk = ukv[:, :, :QK_NOPE].reshape(KV_LORA, N_HEADS * QK_NOPE).astype(BF16)
    wvt = ukv[:, :, QK_NOPE:].reshape(KV_LORA, N_HEADS * V_DIM).T.astype(BF16)
    w_uqt = w_uq[0].T.astype(BF16)
    w_dq_b = w_dq[0].astype(BF16)
    w_ot = w_o[0].T.astype(BF16)
    peer_w = []
    for i in range(2):
        peer_w.append((
            row(norm_ffn[i]),
            peer_w_query[i].T.astype(BF16),
            peer_sub_keys[i].reshape(2 * PEER_HEADS, N_KEYS, D_KEY // 2).astype(BF16),
            peer_u[i].astype(BF16),
            peer_v[i].T.astype(BF16),
        ))
    ple_w = [(row(ple_norm[i]), ple_w_gate[i].astype(BF16), ple_w_proj[i].astype(BF16)) for i in range(2)]

    def run_group(x, p, pos, conv_prev, past_ckv, past_kpe):
        b, t = x.shape[:2]
        n = b * t
        tb = min(512, n)
        x1, conv_state = _conv_mixer(x, conv_prev, row(norm_mix[0]), w_in, conv_kernel[0], w_out, min(512, t))
        xf = x1.reshape(n, d)
        xf = _peer(xf, *peer_w[0], tb=tb)
        xf = _ple(xf, p[0].reshape(n, D_PLE), *ple_w[0], tb)
        ckv, kpe = _latent(xf, row(kv_norm_in), w_lat, row(ckv_norm), g_kpe, _rot_table(pos), tb)
        cos, sin = _rope_tables(pos)
        g_qn = jnp.broadcast_to(qnope_norm[0][:, None], (QK_NOPE, tb))
        g_qr = jnp.broadcast_to(qpe_norm[0][:, None], (QK_ROPE, tb))
        qt = _queries(xf, row(norm_mix[1]), w_dq_b, row(q_a_norm[0]), w_uqt, g_qn, g_qr, cos.T, sin.T, tb)
        tk = 512
        if past_ckv is None:
            k, vt = _expand(ckv, kpe, wk, wvt, row(knope_norm), tk)
            xf = _attention(qt, k, vt, xf, w_ot, tq=tk, tk=tk, rows_out=tk, causal=True, kv_tiles=n // tk)
        else:
            total = past_ckv.shape[1] + t
            tiles = -(-total // tk)
            pad = tiles * tk - total
            c_all = jnp.concatenate([past_ckv, ckv.reshape(b, t, KV_LORA), jnp.zeros((b, pad, KV_LORA), F32)], axis=1)
            r_all = jnp.concatenate([past_kpe, kpe.reshape(b, t, QK_ROPE), jnp.zeros((b, pad, QK_ROPE), F32)], axis=1)
            k, vt = _expand(c_all.reshape(b * tiles * tk, KV_LORA), r_all.reshape(b * tiles * tk, QK_ROPE),
                            wk, wvt, row(knope_norm), tk)
            qt = jnp.pad(qt.reshape(N_HEADS, QK_PAD, b, t), ((0, 0), (0, 0), (0, 0), (0, LANES - t)))
            qt = qt.reshape(N_HEADS, QK_PAD, b * LANES)
            xf = _attention(qt, k, vt, xf, w_ot, tq=LANES, tk=tk, rows_out=t, causal=False, kv_tiles=tiles,
                            tail_valid=tk - pad)
        xf = _peer(xf, *peer_w[1], tb=tb)
        xf = _ple(xf, p[1].reshape(n, D_PLE), *ple_w[1], tb)
        return (xf.reshape(b, t, d), conv_state[None], ckv.reshape(b, t, KV_LORA), kpe.reshape(b, t, QK_ROPE))

    pos_p = jnp.arange(x_prompt.shape[1])
    pos_s = jnp.tile(cache_ckv.shape[1] + jnp.arange(x_sample.shape[1]), x_sample.shape[0])
    zero_conv = jnp.zeros((x_prompt.shape[0], 2, d), F32)
    y_p, conv_p, ckv_p, kpe_p = run_group(x_prompt, p_prompt, pos_p, zero_conv, None, None)
    y_s, conv_s, ckv_s, kpe_s = run_group(x_sample, p_sample, pos_s, state_conv[0], cache_ckv, cache_kpe)
    return (y_p, y_s, conv_p, ckv_p, kpe_p, conv_s, ckv_s, kpe_s)
```
